```python
import jax, jax.numpy as jnp
from jax import lax
import numpy as np

D_MODEL = 1024
BATCH = 16
SEQ = 256
DEPTH = 2
DEC_BATCH = 8
DEC_SEQ = 1024
PAST_LEN = 512

GRID_W = 64
D_RNN = 1024
RNN_BLOCKS = 8
RNN_BW = D_RNN // RNN_BLOCKS
RG_C = 8.0
LRU_CONV_W = 4
LRU_PAD = (2, 1)
N_FOURIER_GROUPS = 4
FOURIER_GW = 128
D_FOURIER = N_FOURIER_GROUPS * FOURIER_GW
D_CONF = 512
CONF_CONV_W = 31
CONF_PAD = (15, 15)
D_SC = 512
SC_CONV_W = 3
SC_PAD = (1, 1)
N_BRANCH = 4
D_FF = 2816
FFN_CONV_W = 3
EPS = 1e-6
F32 = jnp.float32

D_IN = 2 * D_RNN + D_FOURIER + 2 * D_CONF + 3 * D_SC + N_BRANCH * D_MODEL
SPLIT_POINTS = (
    D_RNN,
    2 * D_RNN,
    2 * D_RNN + D_FOURIER,
    2 * D_RNN + D_FOURIER + 2 * D_CONF,
    2 * D_RNN + D_FOURIER + 2 * D_CONF + 3 * D_SC,
)

kernel_name = "hybrid_lru_fourier_conv_dit_step"


def rmsnorm(x, g):
    xf = x.astype(F32)
    y = xf * lax.rsqrt(jnp.mean(xf * xf, axis=-1, keepdims=True) + EPS)
    return (y * g.astype(F32)).astype(x.dtype)


def layernorm(x, g, b):
    xf = x.astype(F32)
    mu = jnp.mean(xf, axis=-1, keepdims=True)
    var = jnp.mean(jnp.square(xf - mu), axis=-1, keepdims=True)
    y = (xf - mu) * lax.rsqrt(var + EPS)
    return (y * g.astype(F32) + b.astype(F32)).astype(x.dtype)


def dwconv1d(x, w, b, pad):
    C = x.shape[-1]
    y = lax.conv_general_dilated(
        x, w[:, None, :].astype(x.dtype), window_strides=(1,), padding=[pad],
        dimension_numbers=("NWC", "WIO", "NWC"), feature_group_count=C)
    return y + b.astype(x.dtype)


def dwconv2d(x, w, b):
    C = x.shape[-1]
    y = lax.conv_general_dilated(
        x, w[:, :, None, :].astype(x.dtype), window_strides=(1, 1), padding=[(1, 1), (1, 1)],
        dimension_numbers=("NHWC", "HWIO", "NHWC"), feature_group_count=C)
    return y + b.astype(x.dtype)


def block_diag(x, w, b):
    xb = x.reshape(x.shape[:-1] + (RNN_BLOCKS, RNN_BW))
    y = jnp.einsum("blhi,hij->blhj", xb, w)
    return y.reshape(x.shape) + b


def _lin_combine(left, right):
    a_l, u_l = left
    a_r, u_r = right
    return a_r * a_l, a_r * u_l + u_r


def rglru_direction(xc, wa, ba, wx, bx, lam, h0, reverse):
    r = jax.nn.sigmoid(block_diag(xc, wa, ba).astype(F32))
    i = jax.nn.sigmoid(block_diag(xc, wx, bx).astype(F32))
    log_a = -RG_C * r * jax.nn.softplus(-lam.astype(F32))
    a = jnp.exp(log_a)
    u = jnp.sqrt(-jnp.expm1(2.0 * log_a)) * (i * xc)
    a_cum, u_cum = lax.associative_scan(_lin_combine, (a, u), reverse=reverse, axis=1)
    h = a_cum * h0[:, None, :] + u_cum
    final = h[:, 0] if reverse else h[:, -1]
    return h, final


def trunk_layer(x, cond, h0, p, grid_rows):
    B, L, _ = x.shape
    dt = x.dtype
    mod = (jax.nn.silu(cond.astype(F32)) @ p["w_mod"].astype(F32) + p["b_mod"].astype(F32)).astype(dt)
    mod = mod.reshape(B, 6, D_MODEL)
    sh1, sc1, g1, sh2, sc2, g2 = [mod[:, k, None, :] for k in range(6)]

    h = rmsnorm(x, p["norm1_g"]) * (1 + sc1) + sh1
    z = h @ p["w_in"]
    zx, zg, zf, zc, zs, zgate = jnp.split(z, SPLIT_POINTS, axis=-1)

    xc = dwconv1d(zx, p["lru_conv_w"], p["lru_conv_b"], LRU_PAD).astype(F32)
    h_f, s_f = rglru_direction(xc, p["lru_wa"][0], p["lru_ba"][0], p["lru_wx"][0], p["lru_bx"][0],
                               p["lru_lam"][0], h0[:, 0], False)
    h_b, s_b = rglru_direction(xc, p["lru_wa"][1], p["lru_ba"][1], p["lru_wx"][1], p["lru_bx"][1],
                               p["lru_lam"][1], h0[:, 1], True)
    y_a = (jax.nn.gelu(zg) * (h_f + h_b).astype(dt)) @ p["lru_out"]

    zf4 = zf.reshape(B, L, N_FOURIER_GROUPS, FOURIER_GW).astype(F32)
    yf = jnp.fft.fft2(zf4, axes=(1, 3), norm="ortho").real.astype(dt)
    y_b = yf.reshape(B, L, D_FOURIER) @ p["fourier_out"]

    za, zb = jnp.split(zc, 2, axis=-1)
    uc = dwconv1d(za * jax.nn.sigmoid(zb), p["conf_dw_w"], p["conf_dw_b"], CONF_PAD)
    y_c = jax.nn.silu(layernorm(uc, p["conf_ln_g"], p["conf_ln_b"])) @ p["conf_out"]

    xs, bs, cs = jnp.split(zs, 3, axis=-1)
    y_d = (bs * dwconv1d(cs * xs, p["sc_conv_w"], p["sc_conv_b"], SC_PAD)) @ p["sc_out"]

    gates = jax.nn.sigmoid(zgate.reshape(B, L, N_BRANCH, D_MODEL) + p["b_gate"])
    branches = jnp.stack([y_a, y_b, y_c, y_d], axis=2)
    merged = jnp.sum(gates * branches, axis=2)
    x = x + g1 * (merged @ p["w_o"])

    h2 = rmsnorm(x, p["norm2_g"]) * (1 + sc2) + sh2
    uf = h2 @ p["ffn_up"]
    if grid_rows is None:
        uf = dwconv1d(uf, p["ffn_conv_w"][1], p["ffn_conv_b"], (1, 1))
    else:
        uf = dwconv2d(uf.reshape(B, grid_rows, GRID_W, 2 * D_FF), p["ffn_conv_w"], p["ffn_conv_b"])
        uf = uf.reshape(B, L, 2 * D_FF)
    ff_a, ff_v = jnp.split(uf, 2, axis=-1)
    x = x + g2 * ((jax.nn.gelu(ff_a) * ff_v) @ p["ffn_down"])
    return x, jnp.stack([s_f, s_b], axis=1)


def setup_inputs(seed: int = 0) -> dict:
    key = jax.random.key(seed)
    ks = iter(jax.random.split(key, 40))
    nrm = lambda shape, s: jax.random.normal(next(ks), shape, F32) * s
    gain = lambda shape: 1.0 + 0.01 * jax.random.normal(next(ks), shape, F32)
    u = jax.random.uniform(next(ks), (DEPTH, 2, D_RNN), F32, 0.9, 0.999)
    a0 = u ** (1.0 / RG_C)
    lam = jnp.log(a0) - jnp.log1p(-a0)
    return {
        "x_prompt": nrm((BATCH, SEQ, D_MODEL), 1.0),
        "x_sample": nrm((DEC_BATCH, DEC_SEQ, D_MODEL), 1.0),
        "state_lru": nrm((DEC_BATCH, DEPTH, 2, D_RNN), 0.5),
        "c": nrm((DEC_BATCH, D_MODEL), 1.0),
        "c_ctx": nrm((D_MODEL,), 1.0),
        "norm1_g": gain((DEPTH, D_MODEL)),
        "norm2_g": gain((DEPTH, D_MODEL)),
        "w_mod": nrm((DEPTH, D_MODEL, 6 * D_MODEL), 0.5 * D_MODEL ** -0.5),
        "b_mod": nrm((DEPTH, 6 * D_MODEL), 0.01),
        "w_in": nrm((DEPTH, D_MODEL, D_IN), D_MODEL ** -0.5),
        "b_gate": nrm((DEPTH, N_BRANCH, D_MODEL), 0.01),
        "lru_conv_w": nrm((DEPTH, LRU_CONV_W, D_RNN), LRU_CONV_W ** -0.5),
        "lru_conv_b": nrm((DEPTH, D_RNN), 0.01),
        "lru_wa": nrm((DEPTH, 2, RNN_BLOCKS, RNN_BW, RNN_BW), RNN_BW ** -0.5),
        "lru_ba": nrm((DEPTH, 2, D_RNN), 0.01),
        "lru_wx": nrm((DEPTH, 2, RNN_BLOCKS, RNN_BW, RNN_BW), RNN_BW ** -0.5),
        "lru_bx": nrm((DEPTH, 2, D_RNN), 0.01),
        "lru_lam": lam,
        "lru_out": nrm((DEPTH, D_RNN, D_MODEL), D_RNN ** -0.5),
        "fourier_out": nrm((DEPTH, D_FOURIER, D_MODEL), D_FOURIER ** -0.5),
        "conf_dw_w": nrm((DEPTH, CONF_CONV_W, D_CONF), CONF_CONV_W ** -0.5),
        "conf_dw_b": nrm((DEPTH, D_CONF), 0.01),
        "conf_ln_g": gain((DEPTH, D_CONF)),
        "conf_ln_b": nrm((DEPTH, D_CONF), 0.01),
        "conf_out": nrm((DEPTH, D_CONF, D_MODEL), D_CONF ** -0.5),
        "sc_conv_w": nrm((DEPTH, SC_CONV_W, D_SC), SC_CONV_W ** -0.5),
        "sc_conv_b": nrm((DEPTH, D_SC), 0.01),
        "sc_out": nrm((DEPTH, D_SC, D_MODEL), D_SC ** -0.5),
        "w_o": nrm((DEPTH, D_MODEL, D_MODEL), D_MODEL ** -0.5),
        "ffn_up": nrm((DEPTH, D_MODEL, 2 * D_FF), D_MODEL ** -0.5),
        "ffn_conv_w": nrm((DEPTH, FFN_CONV_W, FFN_CONV_W, 2 * D_FF), 1.0 / FFN_CONV_W),
        "ffn_conv_b": nrm((DEPTH, 2 * D_FF), 0.01),
        "ffn_down": nrm((DEPTH, D_FF, D_MODEL), D_FF ** -0.5),
        "final_g": gain((D_MODEL,)),
    }


def reference(x_prompt, x_sample, state_lru, c, c_ctx, norm1_g, norm2_g, w_mod, b_mod, w_in, b_gate,
              lru_conv_w, lru_conv_b, lru_wa, lru_ba, lru_wx, lru_bx, lru_lam, lru_out, fourier_out,
              conf_dw_w, conf_dw_b, conf_ln_g, conf_ln_b, conf_out, sc_conv_w, sc_conv_b, sc_out, w_o,
              ffn_up, ffn_conv_w, ffn_conv_b, ffn_down, final_g):
    stacked = {
        "norm1_g": norm1_g, "norm2_g": norm2_g, "w_mod": w_mod, "b_mod": b_mod, "w_in": w_in,
        "b_gate": b_gate, "lru_conv_w": lru_conv_w, "lru_conv_b": lru_conv_b, "lru_wa": lru_wa,
        "lru_ba": lru_ba, "lru_wx": lru_wx, "lru_bx": lru_bx, "lru_lam": lru_lam, "lru_out": lru_out,
        "fourier_out": fourier_out, "conf_dw_w": conf_dw_w, "conf_dw_b": conf_dw_b,
        "conf_ln_g": conf_ln_g, "conf_ln_b": conf_ln_b, "conf_out": conf_out, "sc_conv_w": sc_conv_w,
        "sc_conv_b": sc_conv_b, "sc_out": sc_out, "w_o": w_o, "ffn_up": ffn_up,
        "ffn_conv_w": ffn_conv_w, "ffn_conv_b": ffn_conv_b, "ffn_down": ffn_down,
    }
    n_req = x_prompt.shape[0]
    cond_ctx = jnp.broadcast_to(c_ctx, (n_req, D_MODEL))
    rows = x_sample.shape[1] // GRID_W
    h0_ctx = jnp.zeros((n_req, 2, D_RNN), F32)

    xp = x_prompt
    xs = x_sample
    ctx_states = []
    for l in range(DEPTH):
        p = {k: v[l] for k, v in stacked.items()}
        xp, s_ctx = trunk_layer(xp, cond_ctx, h0_ctx, p, None)
        ctx_states.append(s_ctx)
        xs, _ = trunk_layer(xs, c, state_lru[:, l].astype(F32), p, rows)

    y_prompt = rmsnorm(xp, final_g)
    y_sample = rmsnorm(xs, final_g)
    new_state_lru = jnp.stack(ctx_states, axis=1)
    return (y_prompt, y_sample, new_state_lru)
```

```python
import functools

import numpy as np
import jax
import jax.numpy as jnp
from jax import lax
from jax.experimental import pallas as pl
from jax.experimental.pallas import tpu as pltpu

D_MODEL = 1024
DEPTH = 2
GRID_W = 64
D_RNN = 1024
RNN_BLOCKS = 8
RNN_BW = D_RNN // RNN_BLOCKS
RG_C = 8.0
N_FOURIER_GROUPS = 4
FOURIER_GW = 128
D_FOURIER = N_FOURIER_GROUPS * FOURIER_GW
D_CONF = 512
CONF_CONV_W = 31
D_SC = 512
N_BRANCH = 4
D_FF = 2816
EPS = 1e-6
F32 = jnp.float32
BF16 = jnp.bfloat16

SUBLANES = 8
LANES = 128
NSEG = SUBLANES
ROW_CHUNK = 256
CONV_BLOCKS = 8
FF_BLOCK = 256
VMEM_LIMIT = 60 * 1024 * 1024

OFF_X = 0
OFF_G = D_RNN
OFF_F = 2 * D_RNN
OFF_C = OFF_F + D_FOURIER
OFF_S = OFF_C + 2 * D_CONF
OFF_GATE = OFF_S + 3 * D_SC


def _const_spec(shape):
    nd = len(shape)
    return pl.BlockSpec(shape, lambda *_: (0,) * nd, pipeline_mode=pl.Buffered(1))


def _norm_mod_to(h_s, x_ref, mod_ref, g_ref, nb, L, k_shift, k_scale):
    g = g_ref[...]
    for s in range(nb):
        sh = mod_ref[s, k_shift:k_shift + 1, :]
        sc = mod_ref[s, k_scale:k_scale + 1, :]
        for r0 in range(0, L, ROW_CHUNK):
            x = x_ref[s, r0:r0 + ROW_CHUNK, :]
            ms = jnp.mean(x * x, axis=-1, keepdims=True)
            y = (x * lax.rsqrt(ms + EPS)) * g
            h = y * (1.0 + sc) + sh
            h_s[s * L + r0:s * L + r0 + ROW_CHUNK, :] = h.astype(BF16)


def _fill_halos(ext_s, s, S, pad_l, pad_r):
    C = ext_s.shape[-1]
    step = CONV_BLOCKS * SUBLANES
    n = pad_l * SUBLANES
    for r0 in range(0, n, step):
        m = min(step, n - r0)
        v = ext_s[s, pl.ds(S * SUBLANES - 1 + r0, m), :]
        row = lax.broadcasted_iota(jnp.int32, (m, C), 0)
        ext_s[s, r0:r0 + m, :] = jnp.where(row % SUBLANES != 0, v, 0.0)
    n = pad_r * SUBLANES
    base = (pad_l + S) * SUBLANES
    for r0 in range(0, n, step):
        m = min(step, n - r0)
        v = ext_s[s, pl.ds(pad_l * SUBLANES + 1 + r0, m), :]
        row = lax.broadcasted_iota(jnp.int32, (m, C), 0)
        ext_s[s, base + r0:base + r0 + m, :] = jnp.where(row % SUBLANES != SUBLANES - 1, v, 0.0)


def _conv_chunk(ext_s, s, c0, pad_l, w_ref, taps):
    C = ext_s.shape[-1]
    rows = CONV_BLOCKS * SUBLANES
    acc = None
    for d, k in taps:
        start = pl.multiple_of((c0 + pad_l + d) * SUBLANES, SUBLANES)
        xv = ext_s[s, pl.ds(start, rows), :].reshape(CONV_BLOCKS, SUBLANES, C)
        term = xv * w_ref[k][None]
        acc = term if acc is None else acc + term
    return acc.reshape(rows, C)


def _seg_prefix(P, H, reverse):
    row = lax.broadcasted_iota(jnp.int32, P.shape, 0)
    for k in (1, 2, 4):
        if reverse:
            Ps = pltpu.roll(P, NSEG - k, 0)
            Hs = pltpu.roll(H, NSEG - k, 0)
            m = row < NSEG - k
        else:
            Ps = pltpu.roll(P, k, 0)
            Hs = pltpu.roll(H, k, 0)
            m = row >= k
        H = jnp.where(m, P * Hs + H, H)
        P = jnp.where(m, P * Ps, P)
    return P, H


def _mod_kernel(c_ref, w_ref, b_ref, o_ref):
    c = c_ref[...]
    o_ref[0] = jnp.dot(jax.nn.silu(c), w_ref[0], preferred_element_type=F32,
                       precision=lax.Precision.HIGHEST) + b_ref[0]


def _modulation(cond, w_mod, b_mod):
    R = cond.shape[0]
    nblk = 6
    return pl.pallas_call(
        _mod_kernel,
        grid=(DEPTH, nblk),
        in_specs=[
            pl.BlockSpec((R, D_MODEL), lambda l, j: (0, 0)),
            pl.BlockSpec((1, D_MODEL, D_MODEL), lambda l, j: (l, 0, j)),
            pl.BlockSpec((1, 1, D_MODEL), lambda l, j: (l, 0, j)),
        ],
        out_specs=pl.BlockSpec((1, R, D_MODEL), lambda l, j: (l, 0, j)),
        out_shape=jax.ShapeDtypeStruct((DEPTH, R, 6 * D_MODEL), F32),
        compiler_params=pltpu.CompilerParams(dimension_semantics=("arbitrary", "arbitrary")),
        name="modulation",
    )(cond, w_mod, b_mod.reshape(DEPTH, 1, 6 * D_MODEL))


def _lru_kernel(x_ref, mod_ref, g_ref, wx_ref, wg_ref, wgate_ref, bgate_ref, cw_ref, cb_ref,
                wbd_ref, bbd_ref, lam_ref, h0_ref, wout_ref,
                out_ref, st_ref,
                h_s, ext_s, xc_s, au_s, hs_s, *, nb, L):
    S = L // NSEG
    T = nb * L
    PAD_L, PAD_R = 2, 1
    _norm_mod_to(h_s, x_ref, mod_ref, g_ref, nb, L, 0, 1)

    taps = [(k - PAD_L, k) for k in range(4)]
    for s in range(nb):
        for r0 in range(0, L, ROW_CHUNK):
            zx = jnp.dot(h_s[s * L + r0:s * L + r0 + ROW_CHUNK, :], wx_ref[...],
                         preferred_element_type=F32)
            ext_s[s, PAD_L * SUBLANES + r0:PAD_L * SUBLANES + r0 + ROW_CHUNK, :] = zx
        _fill_halos(ext_s, s, S, PAD_L, PAD_R)

        def conv_body(c, carry, s=s):
            c0 = c * CONV_BLOCKS
            xc = _conv_chunk(ext_s, s, c0, PAD_L, cw_ref, taps) + cb_ref[...]
            r = pl.multiple_of(s * L + c0 * SUBLANES, CONV_BLOCKS * SUBLANES)
            for j in range(RNN_BLOCKS):
                xc_s[j, pl.ds(r, CONV_BLOCKS * SUBLANES), :] = xc[:, j * LANES:(j + 1) * LANES]
            return carry

        lax.fori_loop(0, S // CONV_BLOCKS, conv_body, 0)

    def slab_body(hb, carry):
        W = wbd_ref[hb]
        bb = bbd_ref[hb]
        cneg = []
        for d in range(2):
            lam = lam_ref[hb, d:d + 1, :]
            cneg.append(-RG_C * (jnp.maximum(-lam, 0.0) + jnp.log1p(jnp.exp(-jnp.abs(lam)))))
        for r0 in range(0, T, ROW_CHUNK):
            xr = xc_s[hb, r0:r0 + ROW_CHUNK, :]
            gts = jnp.dot(xr.astype(BF16), W, preferred_element_type=F32) + bb
            for d in range(2):
                r = jax.nn.sigmoid(gts[:, (2 * d) * LANES:(2 * d + 1) * LANES])
                i = jax.nn.sigmoid(gts[:, (2 * d + 1) * LANES:(2 * d + 2) * LANES])
                a = jnp.exp(cneg[d] * r)
                u = jnp.sqrt(1.0 - a * a) * (i * xr)
                au_s[2 * d, r0:r0 + ROW_CHUNK, :] = a
                au_s[2 * d + 1, r0:r0 + ROW_CHUNK, :] = u

        for s in range(nb):
            base = s * L

            def step(i, c, base=base):
                hf, pf, hr, pr = c
                rf = pl.multiple_of(base + i * SUBLANES, SUBLANES)
                a = au_s[0, pl.ds(rf, SUBLANES), :]
                u = au_s[1, pl.ds(rf, SUBLANES), :]
                hf = a * hf + u
                pf = a * pf
                au_s[0, pl.ds(rf, SUBLANES), :] = pf
                au_s[1, pl.ds(rf, SUBLANES), :] = hf
                rr = pl.multiple_of(base + (S - 1 - i) * SUBLANES, SUBLANES)
                a2 = au_s[2, pl.ds(rr, SUBLANES), :]
                u2 = au_s[3, pl.ds(rr, SUBLANES), :]
                hr = a2 * hr + u2
                pr = a2 * pr
                au_s[2, pl.ds(rr, SUBLANES), :] = pr
                au_s[3, pl.ds(rr, SUBLANES), :] = hr
                return hf, pf, hr, pr

            z = jnp.zeros((SUBLANES, LANES), F32)
            o = jnp.ones((SUBLANES, LANES), F32)
            hf, pf, hr, pr = lax.fori_loop(0, S, step, (z, o, z, o), unroll=4)

            row = lax.broadcasted_iota(jnp.int32, (SUBLANES, LANES), 0)
            h0f = h0_ref[s, hb, 0:1, :]
            h0r = h0_ref[s, hb, 1:2, :]
            Pf, Hf = _seg_prefix(pf, hf, False)
            Ef = Pf * h0f + Hf
            cin_f = jnp.where(row == 0, h0f, pltpu.roll(Ef, 1, 0))
            Pr, Hr = _seg_prefix(pr, hr, True)
            Er = Pr * h0r + Hr
            cin_r = jnp.where(row == NSEG - 1, h0r, pltpu.roll(Er, NSEG - 1, 0))
            st_ref[s, hb, 0] = Ef
            st_ref[s, hb, 1] = Er

            def fix(c, carry, base=base, cin_f=cin_f, cin_r=cin_r):
                rows = CONV_BLOCKS * SUBLANES
                r = pl.multiple_of(base + c * rows, rows)
                sl = pl.ds(r, rows)
                shp = (CONV_BLOCKS, SUBLANES, LANES)
                hsum = (au_s[1, sl, :].reshape(shp) + au_s[0, sl, :].reshape(shp) * cin_f[None]
                        + au_s[3, sl, :].reshape(shp) + au_s[2, sl, :].reshape(shp) * cin_r[None])
                hs_s[hb, sl, :] = hsum.reshape(rows, LANES)
                return carry

            lax.fori_loop(0, S // CONV_BLOCKS, fix, 0)
        return carry

    lax.fori_loop(0, RNN_BLOCKS, slab_body, 0)

    for r0 in range(0, T, ROW_CHUNK):
        s, q0 = divmod(r0, L)
        hrow = h_s[r0:r0 + ROW_CHUNK, :]
        zg = jnp.dot(hrow, wg_ref[...], preferred_element_type=F32)
        hsum = jnp.concatenate([hs_s[j, r0:r0 + ROW_CHUNK, :] for j in range(RNN_BLOCKS)], axis=1)
        p = (jax.nn.gelu(zg) * hsum).astype(BF16)
        ya = jnp.dot(p, wout_ref[...], preferred_element_type=F32)
        zgate = jnp.dot(hrow, wgate_ref[...], preferred_element_type=F32) + bgate_ref[...]
        out_ref[s, q0:q0 + ROW_CHUNK, :] = (jax.nn.sigmoid(zgate) * ya).astype(BF16)


def _lru_branch(x, mod, g, wx, wg, wgate, bgate, cw, cb, wbd, bbd, lam, h0, wout, nb):
    B, L, _ = x.shape
    S = L // NSEG
    T = nb * L
    kern = functools.partial(_lru_kernel, nb=nb, L=L)
    tile = lambda i: (i, 0, 0)
    return pl.pallas_call(
        kern,
        grid=(B // nb,),
        in_specs=[
            pl.BlockSpec((nb, L, D_MODEL), tile),
            pl.BlockSpec((nb, 6, D_MODEL), tile),
            _const_spec((1, D_MODEL)),
            _const_spec((D_MODEL, D_RNN)),
            _const_spec((D_MODEL, D_RNN)),
            _const_spec((D_MODEL, D_MODEL)),
            _const_spec((1, D_MODEL)),
            _const_spec((4, SUBLANES, D_RNN)),
            _const_spec((1, D_RNN)),
            _const_spec((RNN_BLOCKS, RNN_BW, 4 * RNN_BW)),
            _const_spec((RNN_BLOCKS, 1, 4 * RNN_BW)),
            _const_spec((RNN_BLOCKS, 2, RNN_BW)),
            pl.BlockSpec((nb, RNN_BLOCKS, 2, RNN_BW), lambda i: (i, 0, 0, 0)),
            _const_spec((D_RNN, D_MODEL)),
        ],
        out_specs=[
            pl.BlockSpec((nb, L, D_MODEL), tile),
            pl.BlockSpec((nb, RNN_BLOCKS, 2, NSEG, RNN_BW), lambda i: (i, 0, 0, 0, 0)),
        ],
        out_shape=[
            jax.ShapeDtypeStruct((B, L, D_MODEL), BF16),
            jax.ShapeDtypeStruct((B, RNN_BLOCKS, 2, NSEG, RNN_BW), F32),
        ],
        scratch_shapes=[
            pltpu.VMEM((T, D_MODEL), BF16),
            pltpu.VMEM((nb, (S + 3) * SUBLANES, D_RNN), F32),
            pltpu.VMEM((RNN_BLOCKS, T, LANES), F32),
            pltpu.VMEM((4, T, LANES), F32),
            pltpu.VMEM((RNN_BLOCKS, T, LANES), F32),
        ],
        compiler_params=pltpu.CompilerParams(dimension_semantics=("arbitrary",),
                                             vmem_limit_bytes=VMEM_LIMIT),
        name="lru_branch",
    )(x, mod, g, wx, wg, wgate, bgate, cw, cb, wbd, bbd, lam, h0, wout)


def _fourier_kernel(x_ref, mod_ref, g_ref, wf_ref, wgate_ref, bgate_ref, cs_ref, dl_ref, wout_ref,
                    out_ref, h_s, zf_s, z_s, *, nb, L):
    _norm_mod_to(h_s, x_ref, mod_ref, g_ref, nb, L, 0, 1)
    for s in range(nb):
        for r0 in range(0, L, ROW_CHUNK):
            zf = jnp.dot(h_s[s * L + r0:s * L + r0 + ROW_CHUNK, :], wf_ref[...],
                         preferred_element_type=F32)
            zf_s[r0:r0 + ROW_CHUNK, :] = zf.astype(BF16)
        for q in range(N_FOURIER_GROUPS):
            lanes = slice(q * FOURIER_GW, (q + 1) * FOURIER_GW)
            for r0 in range(0, L, ROW_CHUNK):
                xcs = jnp.dot(zf_s[r0:r0 + ROW_CHUNK, lanes], cs_ref[...], preferred_element_type=F32)
                z_s[r0:r0 + ROW_CHUNK, lanes] = xcs[:, :FOURIER_GW].astype(BF16)
                z_s[L + r0:L + r0 + ROW_CHUNK, lanes] = xcs[:, FOURIER_GW:].astype(BF16)
        for r0 in range(0, L, ROW_CHUNK):
            yf = jnp.dot(dl_ref[r0:r0 + ROW_CHUNK, :], z_s[...], preferred_element_type=F32)
            yb = jnp.dot(yf.astype(BF16), wout_ref[...], preferred_element_type=F32)
            zgate = jnp.dot(h_s[s * L + r0:s * L + r0 + ROW_CHUNK, :], wgate_ref[...],
                            preferred_element_type=F32) + bgate_ref[...]
            out_ref[s, r0:r0 + ROW_CHUNK, :] = (jax.nn.sigmoid(zgate) * yb).astype(BF16)


def _fourier_branch(x, mod, g, wf, wgate, bgate, cs, dl, wout, nb):
    B, L, _ = x.shape
    T = nb * L
    kern = functools.partial(_fourier_kernel, nb=nb, L=L)
    tile = lambda i: (i, 0, 0)
    return pl.pallas_call(
        kern,
        grid=(B // nb,),
        in_specs=[
            pl.BlockSpec((nb, L, D_MODEL), tile),
            pl.BlockSpec((nb, 6, D_MODEL), tile),
            _const_spec((1, D_MODEL)),
            _const_spec((D_MODEL, D_FOURIER)),
            _const_spec((D_MODEL, D_MODEL)),
            _const_spec((1, D_MODEL)),
            _const_spec((FOURIER_GW, 2 * FOURIER_GW)),
            _const_spec((L, 2 * L)),
            _const_spec((D_FOURIER, D_MODEL)),
        ],
        out_specs=pl.BlockSpec((nb, L, D_MODEL), tile),
        out_shape=jax.ShapeDtypeStruct((B, L, D_MODEL), BF16),
        scratch_shapes=[
            pltpu.VMEM((T, D_MODEL), BF16),
            pltpu.VMEM((L, D_FOURIER), BF16),
            pltpu.VMEM((2 * L, D_FOURIER), BF16),
        ],
        compiler_params=pltpu.CompilerParams(dimension_semantics=("arbitrary",),
                                             vmem_limit_bytes=VMEM_LIMIT),
        name="fourier_branch",
    )(x, mod, g, wf, wgate, bgate, cs, dl, wout)


def _conf_kernel(x_ref, mod_ref, g_ref, wa_ref, wb_ref, wgate_ref, bgate_ref, cw_ref, cb_ref,
                 lng_ref, lnb_ref, wout_ref, out_ref, h_s, ext_s, p_s, *, nb, L):
    S = L // NSEG
    PAD = CONF_CONV_W // 2
    _norm_mod_to(h_s, x_ref, mod_ref, g_ref, nb, L, 0, 1)
    taps = [(k - PAD, k) for k in range(CONF_CONV_W)]
    for s in range(nb):
        for r0 in range(0, L, ROW_CHUNK):
            hrow = h_s[s * L + r0:s * L + r0 + ROW_CHUNK, :]
            za = jnp.dot(hrow, wa_ref[...], preferred_element_type=F32)
            zb = jnp.dot(hrow, wb_ref[...], preferred_element_type=F32)
            ext_s[s, PAD * SUBLANES + r0:PAD * SUBLANES + r0 + ROW_CHUNK, :] = za * jax.nn.sigmoid(zb)
        _fill_halos(ext_s, s, S, PAD, PAD)

        def conv_body(c, carry, s=s):
            c0 = c * CONV_BLOCKS
            uc = _conv_chunk(ext_s, s, c0, PAD, cw_ref, taps) + cb_ref[...]
            mu = jnp.mean(uc, axis=-1, keepdims=True)
            dv = uc - mu
            var = jnp.mean(dv * dv, axis=-1, keepdims=True)
            y = dv * lax.rsqrt(var + EPS) * lng_ref[...] + lnb_ref[...]
            r = pl.multiple_of(s * L + c0 * SUBLANES, CONV_BLOCKS * SUBLANES)
            p_s[pl.ds(r, CONV_BLOCKS * SUBLANES), :] = jax.nn.silu(y).astype(BF16)
            return carry

        lax.fori_loop(0, S // CONV_BLOCKS, conv_body, 0)

    for r0 in range(0, nb * L, ROW_CHUNK):
        s, q0 = divmod(r0, L)
        yc = jnp.dot(p_s[r0:r0 + ROW_CHUNK, :], wout_ref[...], preferred_element_type=F32)
        zgate = jnp.dot(h_s[r0:r0 + ROW_CHUNK, :], wgate_ref[...], preferred_element_type=F32) + bgate_ref[...]
        out_ref[s, q0:q0 + ROW_CHUNK, :] = (jax.nn.sigmoid(zgate) * yc).astype(BF16)


def _conf_branch(x, mod, g, wa, wb, wgate, bgate, cw, cb, lng, lnb, wout, nb):
    B, L, _ = x.shape
    S = L // NSEG
    T = nb * L
    kern = functools.partial(_conf_kernel, nb=nb, L=L)
    tile = lambda i: (i, 0, 0)
    return pl.pallas_call(
        kern,
        grid=(B // nb,),
        in_specs=[
            pl.BlockSpec((nb, L, D_MODEL), tile),
            pl.BlockSpec((nb, 6, D_MODEL), tile),
            _const_spec((1, D_MODEL)),
            _const_spec((D_MODEL, D_CONF)),
            _const_spec((D_MODEL, D_CONF)),
            _const_spec((D_MODEL, D_MODEL)),
            _const_spec((1, D_MODEL)),
            _const_spec((CONF_CONV_W, SUBLANES, D_CONF)),
            _const_spec((1, D_CONF)),
            _const_spec((1, D_CONF)),
            _const_spec((1, D_CONF)),
            _const_spec((D_CONF, D_MODEL)),
        ],
        out_specs=pl.BlockSpec((nb, L, D_MODEL), tile),
        out_shape=jax.ShapeDtypeStruct((B, L, D_MODEL), BF16),
        scratch_shapes=[
            pltpu.VMEM((T, D_MODEL), BF16),
            pltpu.VMEM((nb, (S + CONF_CONV_W - 1) * SUBLANES, D_CONF), F32),
            pltpu.VMEM((T, D_CONF), BF16),
        ],
        compiler_params=pltpu.CompilerParams(dimension_semantics=("arbitrary",),
                                             vmem_limit_bytes=VMEM_LIMIT),
        name="conf_branch",
    )(x, mod, g, wa, wb, wgate, bgate, cw, cb, lng, lnb, wout)


def _merge_kernel(x_ref, mod_ref, g_ref, ya_ref, yb_ref, yc_ref, ws_ref, wgate_ref, bgate_ref,
                  cw_ref, cb_ref, wsc_ref, wo_ref, out_ref, h_s, ext_s, bs_s, p_s, *, nb, L):
    S = L // NSEG
    _norm_mod_to(h_s, x_ref, mod_ref, g_ref, nb, L, 0, 1)
    taps = [(k - 1, k) for k in range(3)]
    for s in range(nb):
        for r0 in range(0, L, ROW_CHUNK):
            zs = jnp.dot(h_s[s * L + r0:s * L + r0 + ROW_CHUNK, :], ws_ref[...], preferred_element_type=F32)
            xs = zs[:, :D_SC]
            bs = zs[:, D_SC:2 * D_SC]
            cs = zs[:, 2 * D_SC:]
            ext_s[s, SUBLANES + r0:SUBLANES + r0 + ROW_CHUNK, :] = cs * xs
            bs_s[s * L + r0:s * L + r0 + ROW_CHUNK, :] = bs
        _fill_halos(ext_s, s, S, 1, 1)

        def conv_body(c, carry, s=s):
            c0 = c * CONV_BLOCKS
            v = _conv_chunk(ext_s, s, c0, 1, cw_ref, taps) + cb_ref[...]
            r = pl.multiple_of(s * L + c0 * SUBLANES, CONV_BLOCKS * SUBLANES)
            sl = pl.ds(r, CONV_BLOCKS * SUBLANES)
            p_s[sl, :] = (bs_s[sl, :] * v).astype(BF16)
            return carry

        lax.fori_loop(0, S // CONV_BLOCKS, conv_body, 0)

    for r0 in range(0, nb * L, ROW_CHUNK):
        s, q0 = divmod(r0, L)
        rows = slice(q0, q0 + ROW_CHUNK)
        yd = jnp.dot(p_s[r0:r0 + ROW_CHUNK, :], wsc_ref[...], preferred_element_type=F32)
        zgate = jnp.dot(h_s[r0:r0 + ROW_CHUNK, :], wgate_ref[...], preferred_element_type=F32) + bgate_ref[...]
        merged = (ya_ref[s, rows, :].astype(F32) + yb_ref[s, rows, :].astype(F32)
                  + yc_ref[s, rows, :].astype(F32) + jax.nn.sigmoid(zgate) * yd)
        o = jnp.dot(merged.astype(BF16), wo_ref[...], preferred_element_type=F32)
        out_ref[s, rows, :] = x_ref[s, rows, :] + mod_ref[s, 2:3, :] * o


def _merge_branch(x, mod, g, ya, yb, yc, ws, wgate, bgate, cw, cb, wsc, wo, nb):
    B, L, _ = x.shape
    S = L // NSEG
    T = nb * L
    kern = functools.partial(_merge_kernel, nb=nb, L=L)
    tile = lambda i: (i, 0, 0)
    act = pl.BlockSpec((nb, L, D_MODEL), tile)
    return pl.pallas_call(
        kern,
        grid=(B // nb,),
        in_specs=[
            act,
            pl.BlockSpec((nb, 6, D_MODEL), tile),
            _const_spec((1, D_MODEL)),
            act, act, act,
            _const_spec((D_MODEL, 3 * D_SC)),
            _const_spec((D_MODEL, D_MODEL)),
            _const_spec((1, D_MODEL)),
            _const_spec((3, SUBLANES, D_SC)),
            _const_spec((1, D_SC)),
            _const_spec((D_SC, D_MODEL)),
            _const_spec((D_MODEL, D_MODEL)),
        ],
        out_specs=act,
        out_shape=jax.ShapeDtypeStruct((B, L, D_MODEL), F32),
        scratch_shapes=[
            pltpu.VMEM((T, D_MODEL), BF16),
            pltpu.VMEM((nb, (S + 2) * SUBLANES, D_SC), F32),
            pltpu.VMEM((T, D_SC), F32),
            pltpu.VMEM((T, D_SC), BF16),
        ],
        compiler_params=pltpu.CompilerParams(dimension_semantics=("arbitrary",),
                                             vmem_limit_bytes=VMEM_LIMIT),
        name="merge_branch",
    )(x, mod, g, ya, yb, yc, ws, wgate, bgate, cw, cb, wsc, wo)


def _ffn_kernel(x_ref, mod_ref, g_ref, up_ref, cw_ref, cb_ref, down_ref, fg_ref, out_ref,
                h_s, acc_s, ext_s, p_s, *, nb, L, grid_w, final):
    S = L // NSEG
    T = nb * L
    j = pl.program_id(1)
    nj = pl.num_programs(1)
    C2 = 2 * FF_BLOCK
    rows = CONV_BLOCKS * SUBLANES

    @pl.when(j == 0)
    def _():
        _norm_mod_to(h_s, x_ref, mod_ref, g_ref, nb, L, 3, 4)
        acc_s[...] = jnp.zeros_like(acc_s)

    if grid_w is None:
        pad = 1
        groups = {0: [(0, 4)], -1: [(-1, 3)], 1: [(1, 5)]}
    else:
        pad = grid_w + 1
        groups = {dx: [(dy * grid_w + dx, (dy + 1) * 3 + dx + 1) for dy in (-1, 0, 1)] for dx in (-1, 0, 1)}

    for s in range(nb):
        for r0 in range(0, L, ROW_CHUNK):
            uf = jnp.dot(h_s[s * L + r0:s * L + r0 + ROW_CHUNK, :], up_ref[0], preferred_element_type=F32)
            ext_s[s, pad * SUBLANES + r0:pad * SUBLANES + r0 + ROW_CHUNK, :] = uf
        _fill_halos(ext_s, s, S, pad, pad)

        def conv_body(c, carry, s=s):
            c0 = c * CONV_BLOCKS
            v = _conv_chunk(ext_s, s, c0, pad, cw_ref.at[0], groups[0])
            left = _conv_chunk(ext_s, s, c0, pad, cw_ref.at[0], groups[-1])
            right = _conv_chunk(ext_s, s, c0, pad, cw_ref.at[0], groups[1])
            if grid_w is not None:
                step = c0 + lax.broadcasted_iota(jnp.int32, (rows, C2), 0) // SUBLANES
                col = step % grid_w
                left = jnp.where(col != 0, left, 0.0)
                right = jnp.where(col != grid_w - 1, right, 0.0)
            v = v + left + right + cb_ref[0]
            r = pl.multiple_of(s * L + c0 * SUBLANES, rows)
            p_s[pl.ds(r, rows), :] = (jax.nn.gelu(v[:, :FF_BLOCK]) * v[:, FF_BLOCK:]).astype(BF16)
            return carry

        lax.fori_loop(0, S // CONV_BLOCKS, conv_body, 0)

    for r0 in range(0, T, ROW_CHUNK):
        acc_s[r0:r0 + ROW_CHUNK, :] += jnp.dot(p_s[r0:r0 + ROW_CHUNK, :], down_ref[0],
                                               preferred_element_type=F32)

    @pl.when(j == nj - 1)
    def _():
        for r0 in range(0, T, ROW_CHUNK):
            s, q0 = divmod(r0, L)
            sl = slice(q0, q0 + ROW_CHUNK)
            y = x_ref[s, sl, :] + mod_ref[s, 5:6, :] * acc_s[r0:r0 + ROW_CHUNK, :]
            if final:
                ms = jnp.mean(y * y, axis=-1, keepdims=True)
                y = (y * lax.rsqrt(ms + EPS)) * fg_ref[...]
            out_ref[s, sl, :] = y


def _ffn(x, mod, g, up, cw, cb, down, fg, nb, grid_w, final):
    B, L, _ = x.shape
    S = L // NSEG
    T = nb * L
    nj = D_FF // FF_BLOCK
    ntap = cw.shape[1]
    pad = 1 if grid_w is None else grid_w + 1
    kern = functools.partial(_ffn_kernel, nb=nb, L=L, grid_w=grid_w, final=final)
    tile = lambda i, j: (i, 0, 0)
    return pl.pallas_call(
        kern,
        grid=(B // nb, nj),
        in_specs=[
            pl.BlockSpec((nb, L, D_MODEL), tile),
            pl.BlockSpec((nb, 6, D_MODEL), tile),
            pl.BlockSpec((1, D_MODEL), lambda i, j: (0, 0)),
            pl.BlockSpec((1, D_MODEL, 2 * FF_BLOCK), lambda i, j: (j, 0, 0)),
            pl.BlockSpec((1, ntap, SUBLANES, 2 * FF_BLOCK), lambda i, j: (j, 0, 0, 0)),
            pl.BlockSpec((1, 1, 2 * FF_BLOCK), lambda i, j: (j, 0, 0)),
            pl.BlockSpec((1, FF_BLOCK, D_MODEL), lambda i, j: (j, 0, 0)),
            pl.BlockSpec((1, D_MODEL), lambda i, j: (0, 0)),
        ],
        out_specs=pl.BlockSpec((nb, L, D_MODEL), tile),
        out_shape=jax.ShapeDtypeStruct((B, L, D_MODEL), F32),
        scratch_shapes=[
            pltpu.VMEM((T, D_MODEL), BF16),
            pltpu.VMEM((T, D_MODEL), F32),
            pltpu.VMEM((nb, (S + 2 * pad) * SUBLANES, 2 * FF_BLOCK), F32),
            pltpu.VMEM((T, FF_BLOCK), BF16),
        ],
        compiler_params=pltpu.CompilerParams(dimension_semantics=("arbitrary", "arbitrary"),
                                             vmem_limit_bytes=VMEM_LIMIT),
        name="conv_ffn",
    )(x, mod, g, up, cw, cb, down, fg)


def _to_seg(x):
    B, L, D = x.shape
    return x.reshape(B, NSEG, L // NSEG, D).transpose(0, 2, 1, 3).reshape(B, L, D)


def _from_seg(x):
    B, L, D = x.shape
    return x.reshape(B, L // NSEG, NSEG, D).transpose(0, 2, 1, 3).reshape(B, L, D)


def _dft_constants(L):
    S = L // NSEG
    r = np.arange(L)
    t = (r % NSEG) * S + r // NSEG
    ang = 2.0 * np.pi * ((t[:, None] * t[None, :]) % L) / L
    dl = np.concatenate([np.cos(ang), -np.sin(ang)], axis=1) / np.sqrt(L)
    c = np.arange(FOURIER_GW)
    angc = 2.0 * np.pi * ((c[:, None] * c[None, :]) % FOURIER_GW) / FOURIER_GW
    cs = np.concatenate([np.cos(angc), np.sin(angc)], axis=1) / np.sqrt(FOURIER_GW)
    return jnp.asarray(cs, BF16), jnp.asarray(dl, BF16)


def _bcast_taps(w):
    return jnp.broadcast_to(w[:, None, :], (w.shape[0], SUBLANES, w.shape[1]))


def _trunk_layer(x, mod, h0, p, cs, dl, nb, grid_w, final):
    g1 = p["norm1_g"].reshape(1, D_MODEL)
    gate = lambda k: p["w_in"][:, OFF_GATE + k * D_MODEL:OFF_GATE + (k + 1) * D_MODEL].astype(BF16)
    bgate = lambda k: p["b_gate"][k].reshape(1, D_MODEL)
    w_in = p["w_in"]

    wbd = jnp.concatenate([p["lru_wa"][0], p["lru_wx"][0], p["lru_wa"][1], p["lru_wx"][1]], axis=-1).astype(BF16)
    blk = lambda b: b.reshape(RNN_BLOCKS, 1, RNN_BW)
    bbd = jnp.concatenate([blk(p["lru_ba"][0]), blk(p["lru_bx"][0]), blk(p["lru_ba"][1]), blk(p["lru_bx"][1])],
                          axis=-1)
    lam = p["lru_lam"].reshape(2, RNN_BLOCKS, RNN_BW).transpose(1, 0, 2)
    h0b = h0.reshape(h0.shape[0], 2, RNN_BLOCKS, RNN_BW).transpose(0, 2, 1, 3)
    ya, st = _lru_branch(x, mod, g1, w_in[:, OFF_X:OFF_G].astype(BF16), w_in[:, OFF_G:OFF_F].astype(BF16),
                         gate(0), bgate(0), _bcast_taps(p["lru_conv_w"]), p["lru_conv_b"].reshape(1, D_RNN),
                         wbd, bbd, lam, h0b, p["lru_out"].astype(BF16), nb)
    st = jnp.stack([st[:, :, 0, NSEG - 1, :], st[:, :, 1, 0, :]], axis=1).reshape(st.shape[0], 2, D_RNN)
    yb = _fourier_branch(x, mod, g1, w_in[:, OFF_F:OFF_C].astype(BF16), gate(1), bgate(1), cs, dl,
                         p["fourier_out"].astype(BF16), nb)
    yc = _conf_branch(x, mod, g1, w_in[:, OFF_C:OFF_C + D_CONF].astype(BF16),
                      w_in[:, OFF_C + D_CONF:OFF_S].astype(BF16), gate(2), bgate(2),
                      _bcast_taps(p["conf_dw_w"]), p["conf_dw_b"].reshape(1, D_CONF),
                      p["conf_ln_g"].reshape(1, D_CONF), p["conf_ln_b"].reshape(1, D_CONF),
                      p["conf_out"].astype(BF16), nb)
    x = _merge_branch(x, mod, g1, ya, yb, yc, w_in[:, OFF_S:OFF_GATE].astype(BF16), gate(3), bgate(3),
                      _bcast_taps(p["sc_conv_w"]), p["sc_conv_b"].reshape(1, D_SC),
                      p["sc_out"].astype(BF16), p["w_o"].astype(BF16), nb)
    nj = D_FF // FF_BLOCK
    pair = lambda w: jnp.concatenate([w[..., :D_FF].reshape(w.shape[:-1] + (nj, FF_BLOCK)),
                                      w[..., D_FF:].reshape(w.shape[:-1] + (nj, FF_BLOCK))], axis=-1)
    up = pair(p["ffn_up"]).transpose(1, 0, 2).astype(BF16)
    cw = pair(_bcast_taps(p["ffn_conv_w"].reshape(9, 2 * D_FF))).transpose(2, 0, 1, 3)
    cb = pair(p["ffn_conv_b"].reshape(1, 2 * D_FF)).transpose(1, 0, 2)
    down = p["ffn_down"].reshape(nj, FF_BLOCK, D_MODEL).astype(BF16)
    x = _ffn(x, mod, p["norm2_g"].reshape(1, D_MODEL), up, cw, cb, down, p["final_g"].reshape(1, D_MODEL),
             nb, grid_w, final)
    return x, st


def kernel(x_prompt, x_sample, state_lru, c, c_ctx, norm1_g, norm2_g, w_mod, b_mod, w_in, b_gate,
           lru_conv_w, lru_conv_b, lru_wa, lru_ba, lru_wx, lru_bx, lru_lam, lru_out, fourier_out,
           conf_dw_w, conf_dw_b, conf_ln_g, conf_ln_b, conf_out, sc_conv_w, sc_conv_b, sc_out, w_o,
           ffn_up, ffn_conv_w, ffn_conv_b, ffn_down, final_g):
    stacked = {
        "norm1_g": norm1_g, "norm2_g": norm2_g, "w_in": w_in,
        "b_gate": b_gate, "lru_conv_w": lru_conv_w, "lru_conv_b": lru_conv_b, "lru_wa": lru_wa,
        "lru_ba": lru_ba, "lru_wx": lru_wx, "lru_bx": lru_bx, "lru_lam": lru_lam, "lru_out": lru_out,
        "fourier_out": fourier_out, "conf_dw_w": conf_dw_w, "conf_dw_b": conf_dw_b,
        "conf_ln_g": conf_ln_g, "conf_ln_b": conf_ln_b, "conf_out": conf_out, "sc_conv_w": sc_conv_w,
        "sc_conv_b": sc_conv_b, "sc_out": sc_out, "w_o": w_o, "ffn_up": ffn_up,
        "ffn_conv_w": ffn_conv_w, "ffn_conv_b": ffn_conv_b, "ffn_down": ffn_down,
    }
    n_req, seq, _ = x_prompt.shape
    n_dec, dec_seq, _ = x_sample.shape

    n_rows = -(-(n_dec + 1) // SUBLANES) * SUBLANES
    cond = jnp.zeros((n_rows, D_MODEL), F32).at[:n_dec].set(c).at[n_dec].set(c_ctx)
    mod = _modulation(cond, w_mod, b_mod)

    cs_p, dl_p = _dft_constants(seq)
    cs_s, dl_s = _dft_constants(dec_seq)

    xp = _to_seg(x_prompt)
    xs = _to_seg(x_sample)
    h0_ctx = jnp.zeros((n_req, 2, D_RNN), F32)
    nb_ctx = max(1, min(n_req, dec_seq // seq))
    states = []
    for l in range(DEPTH):
        p = {k: v[l] for k, v in stacked.items()}
        p["final_g"] = final_g
        mod_lat = mod[l, :n_dec].reshape(n_dec, 6, D_MODEL)
        mod_ctx = jnp.broadcast_to(mod[l, n_dec].reshape(1, 6, D_MODEL), (n_req, 6, D_MODEL))
        final = l == DEPTH - 1
        xp, st = _trunk_layer(xp, mod_ctx, h0_ctx, p, cs_p, dl_p, nb_ctx, None, final)
        states.append(st)
        xs, _ = _trunk_layer(xs, mod_lat, state_lru[:, l].astype(F32), p, cs_s, dl_s, 1, GRID_W, final)

    return (_from_seg(xp), _from_seg(xs), jnp.stack(states, axis=1))
```
